```python
import math
import jax, jax.numpy as jnp
from jax import lax
import numpy as np

D_MODEL = 2048
BATCH = 2
SEQ = 8192
DEPTH = 4
DEC_BATCH = 16
DEC_SEQ = 2048
PAST_LEN = 128

N_MIXERS = 4
LAYERS_PER_MIXER = DEPTH // N_MIXERS
NORM_EPS = 1e-6

SSD_DI = 2 * D_MODEL
SSD_HEADDIM = 64
SSD_NH = SSD_DI // SSD_HEADDIM
SSD_G = 8
SSD_N = 128
SSD_CONV = 7
SSD_CHUNK = 128
SSD_CONV_CH = SSD_DI + 2 * SSD_G * SSD_N
SSD_IN = SSD_DI + SSD_CONV_CH + 2 * SSD_NH

ATT_H = 8
ATT_DH = D_MODEL // (2 * ATT_H)
ATT_VD = 2 * ATT_DH
ATT_QBLOCK = 128
ATT_IN = 2 * ATT_H * ATT_DH * 2 + ATT_H * ATT_VD
REL_BUCKETS = 32
REL_MAX_DIST = 128

ML_H = 8
ML_DQK = 128
ML_DV = 256
ML_CHUNK = 128
ML_QK = ML_H * ML_DQK
ML_V = ML_H * ML_DV
ML_IN = 2 * ML_QK + 2 * ML_V + 4 * ML_H

HG_H = 16
HG_DK = 128
HG_DV = 128
HG_CHUNK = 64
HG_K = HG_H * HG_DK
HG_V = HG_H * HG_DV
HG_IN = 3 * HG_K + 2 * HG_V

FFN_D = 5632
FFN_CONV = 3

kernel_name = "hybrid_bidir_ssd_diffattn_mlstm_hgrn2_encoder"


def _rmsnorm(x, w):
    xf = x.astype(jnp.float32)
    y = xf * lax.rsqrt(jnp.mean(xf * xf, axis=-1, keepdims=True) + NORM_EPS)
    return (y * w.astype(jnp.float32)).astype(x.dtype)


def _dwconv(x, w, b):
    width = w.shape[0]
    y = lax.conv_general_dilated(
        x, w[:, None, :].astype(x.dtype), window_strides=(1,),
        padding=[(width // 2, width // 2)],
        dimension_numbers=('NWC', 'WIO', 'NWC'),
        feature_group_count=x.shape[-1])
    return y + b.astype(x.dtype)


def _flip(t):
    return jnp.flip(t, axis=1)


def _to_chunks(t, n):
    return t.reshape(t.shape[0], t.shape[1] // n, n, *t.shape[2:])


def _ssd_scan(xdt, a, bm, cm):
    bsz, s = xdt.shape[:2]
    hg = SSD_NH // SSD_G
    L = SSD_CHUNK
    nc = s // L
    xc = _to_chunks(xdt, L).reshape(bsz, nc, L, SSD_G, hg, SSD_HEADDIM)
    ac = _to_chunks(a, L).reshape(bsz, nc, L, SSD_G, hg)
    bc = _to_chunks(bm, L)
    cc = _to_chunks(cm, L)
    a_cum = jnp.cumsum(ac, axis=2)
    causal = jnp.tril(jnp.ones((L, L), bool))
    seg = a_cum[:, :, :, None] - a_cum[:, :, None, :]
    decay_ts = jnp.exp(jnp.where(causal[None, None, :, :, None, None], seg, -jnp.inf))
    cb = jnp.einsum('bctgn,bcsgn->bctsg', cc, bc)
    y_diag = jnp.einsum('bctsg,bctsgh,bcsghp->bctghp', cb, decay_ts, xc)
    decay_end = jnp.exp(a_cum[:, :, -1:] - a_cum)
    states = jnp.einsum('bcsgn,bcsgh,bcsghp->bcghpn', bc, decay_end, xc)
    chunk_decay = jnp.exp(a_cum[:, :, -1])

    def step(h, inp):
        st, dec = inp
        return h * dec[..., None, None] + st, h

    h0 = jnp.zeros((bsz, SSD_G, hg, SSD_HEADDIM, SSD_N), jnp.float32)
    _, prev = lax.scan(step, h0, (jnp.moveaxis(states, 1, 0), jnp.moveaxis(chunk_decay, 1, 0)))
    prev = jnp.moveaxis(prev, 0, 1)
    y_off = jnp.einsum('bctgn,bctgh,bcghpn->bctghp', cc, jnp.exp(a_cum), prev)
    return (y_diag + y_off).reshape(bsz, s, SSD_NH, SSD_HEADDIM)


def _ssd_mixer(h, w_in, conv_w, conv_b, dt_bias, a_log, d_skip, norm_w, w_out):
    bsz, s, _ = h.shape
    f32 = jnp.float32
    proj = h @ w_in
    z, xbc, dt = jnp.split(proj, [SSD_DI, SSD_DI + SSD_CONV_CH], axis=-1)
    xbc = jax.nn.silu(_dwconv(xbc, conv_w, conv_b)).astype(f32)
    xs, bm, cm = jnp.split(xbc, [SSD_DI, SSD_DI + SSD_G * SSD_N], axis=-1)
    xs = xs.reshape(bsz, s, SSD_NH, SSD_HEADDIM)
    bm = bm.reshape(bsz, s, SSD_G, SSD_N)
    cm = cm.reshape(bsz, s, SSD_G, SSD_N)
    dt = jax.nn.softplus(dt.astype(f32).reshape(bsz, s, 2, SSD_NH) + dt_bias.astype(f32))
    a = -jnp.exp(a_log.astype(f32)) * dt
    y_fwd = _ssd_scan(xs * dt[:, :, 0, :, None], a[:, :, 0], bm, cm)
    y_bwd = _flip(_ssd_scan(_flip(xs * dt[:, :, 1, :, None]), _flip(a[:, :, 1]), _flip(bm), _flip(cm)))
    y = y_fwd + y_bwd + d_skip.astype(f32)[:, None] * xs
    y = y.reshape(bsz, s, SSD_DI) * jax.nn.silu(z.astype(f32))
    y = _rmsnorm(y, norm_w)
    return y.astype(h.dtype) @ w_out


def _rel_bucket(rel):
    half = REL_BUCKETS // 2
    max_exact = half // 2
    ret = jnp.where(rel > 0, half, 0)
    n = jnp.abs(rel)
    nf = jnp.maximum(n, 1).astype(jnp.float32)
    large = max_exact + (jnp.log(nf / max_exact) / math.log(REL_MAX_DIST / max_exact)
                         * (half - max_exact)).astype(jnp.int32)
    large = jnp.minimum(large, half - 1)
    return ret + jnp.where(n < max_exact, n, large)


def _diff_attn_mixer(h, w_qkv, q_norm, k_norm, lam_p, sub_norm, w_out, rel_bias, lambda_init):
    bsz, s, _ = h.shape
    f32 = jnp.float32
    qkv = h @ w_qkv
    q, k, v = jnp.split(qkv, [2 * ATT_H * ATT_DH, 4 * ATT_H * ATT_DH], axis=-1)
    q = _rmsnorm(q.reshape(bsz, s, ATT_H, 2, ATT_DH), q_norm)
    k = _rmsnorm(k.reshape(bsz, s, ATT_H, 2, ATT_DH), k_norm)
    v = v.reshape(bsz, s, ATT_H, ATT_VD)
    lp = lam_p.astype(f32)
    lam = jnp.exp(jnp.sum(lp[0] * lp[1])) - jnp.exp(jnp.sum(lp[2] * lp[3])) + lambda_init
    nb = s // ATT_QBLOCK
    q_blocks = jnp.moveaxis(q.reshape(bsz, nb, ATT_QBLOCK, ATT_H, 2, ATT_DH), 1, 0)
    k_pos = jnp.arange(s, dtype=jnp.int32)
    table = rel_bias.astype(f32)
    scale = ATT_DH ** -0.5

    def block(args):
        qb, bi = args
        q_pos = bi * ATT_QBLOCK + jnp.arange(ATT_QBLOCK, dtype=jnp.int32)
        bias = jnp.transpose(table[_rel_bucket(k_pos[None, :] - q_pos[:, None])], (2, 0, 1))
        logits = jnp.einsum('bqhjd,bkhjd->bjhqk', qb, k).astype(f32) * scale + bias[None, None]
        p = jax.nn.softmax(logits, axis=-1)
        attn = (p[:, 0] - lam * p[:, 1]).astype(v.dtype)
        return jnp.einsum('bhqk,bkhe->bqhe', attn, v)

    o = lax.map(block, (q_blocks, jnp.arange(nb, dtype=jnp.int32)))
    o = jnp.moveaxis(o, 0, 1).reshape(bsz, s, ATT_H, ATT_VD)
    o = _rmsnorm(o, sub_norm) * (1.0 - lambda_init)
    return o.reshape(bsz, s, ATT_H * ATT_VD).astype(h.dtype) @ w_out


def _mlstm_scan(q, k, v, ig, fg):
    bsz, s = q.shape[:2]
    L = ML_CHUNK
    qc, kc, vc = _to_chunks(q, L), _to_chunks(k, L), _to_chunks(v, L)
    igc = _to_chunks(ig, L)
    bcum = jnp.cumsum(_to_chunks(jax.nn.log_sigmoid(fg), L), axis=2)
    g_end = bcum[:, :, -1]
    causal = jnp.tril(jnp.ones((L, L), bool))
    d_ts = jnp.where(causal[None, None, :, :, None],
                     bcum[:, :, :, None] - bcum[:, :, None, :] + igc[:, :, None, :], -jnp.inf)
    w_s = g_end[:, :, None] - bcum + igc

    def step(carry, inp):
        c_st, n_st, m_st = carry
        kk, vv, ws, ge = inp
        m_new = jnp.maximum(ge + m_st, jnp.max(ws, axis=1))
        decay = jnp.exp(ge + m_st - m_new)
        wts = jnp.exp(ws - m_new[:, None])
        c_new = decay[..., None, None] * c_st + jnp.einsum('bsh,bshk,bshv->bhkv', wts, kk, vv)
        n_new = decay[..., None] * n_st + jnp.einsum('bsh,bshk->bhk', wts, kk)
        return (c_new, n_new, m_new), (c_st, n_st, m_st)

    init = (jnp.zeros((bsz, ML_H, ML_DQK, ML_DV), jnp.float32),
            jnp.zeros((bsz, ML_H, ML_DQK), jnp.float32),
            jnp.zeros((bsz, ML_H), jnp.float32))
    xs = (jnp.moveaxis(kc, 1, 0), jnp.moveaxis(vc, 1, 0), jnp.moveaxis(w_s, 1, 0), jnp.moveaxis(g_end, 1, 0))
    _, (c_prev, n_prev, m_prev) = lax.scan(step, init, xs)
    c_prev = jnp.moveaxis(c_prev, 0, 1)
    n_prev = jnp.moveaxis(n_prev, 0, 1)
    m_prev = jnp.moveaxis(m_prev, 0, 1)
    a_t = bcum + m_prev[:, :, None]
    m_t = jnp.maximum(a_t, jnp.max(d_ts, axis=3))
    scores = jnp.einsum('bcthk,bcshk->bctsh', qc, kc) * jnp.exp(d_ts - m_t[:, :, :, None])
    inter = jnp.exp(a_t - m_t)
    num = (jnp.einsum('bctsh,bcshv->bcthv', scores, vc)
           + inter[..., None] * jnp.einsum('bcthk,bchkv->bcthv', qc, c_prev))
    den = jnp.sum(scores, axis=3) + inter * jnp.einsum('bcthk,bchk->bcth', qc, n_prev)
    out = num / jnp.maximum(jnp.abs(den), jnp.exp(-m_t))[..., None]
    return out.reshape(bsz, s, ML_H, ML_DV)


def _mlstm_mixer(h, w_in, gate_b, norm_w, w_out):
    bsz, s, _ = h.shape
    proj = (h @ w_in).astype(jnp.float32)
    q, k, v, o, gates = jnp.split(proj, [ML_QK, 2 * ML_QK, 2 * ML_QK + ML_V, 2 * ML_QK + 2 * ML_V], axis=-1)
    q = q.reshape(bsz, s, ML_H, ML_DQK)
    k = k.reshape(bsz, s, ML_H, ML_DQK) * (ML_DQK ** -0.5)
    v = v.reshape(bsz, s, ML_H, ML_DV)
    gates = gates.reshape(bsz, s, 4, ML_H) + gate_b.astype(jnp.float32)
    h_fwd = _mlstm_scan(q, k, v, gates[:, :, 0], gates[:, :, 1])
    h_bwd = _flip(_mlstm_scan(_flip(q), _flip(k), _flip(v), _flip(gates[:, :, 2]), _flip(gates[:, :, 3])))
    y = _rmsnorm(h_fwd + h_bwd, norm_w.reshape(ML_H, ML_DV)).reshape(bsz, s, ML_V)
    y = y * jax.nn.sigmoid(o)
    return y.astype(h.dtype) @ w_out


def _hgrn_scan(q, k, logf, v):
    bsz, s = q.shape[:2]
    L = HG_CHUNK
    qc, kc, fc, vc = _to_chunks(q, L), _to_chunks(k, L), _to_chunks(logf, L), _to_chunks(v, L)
    bcum = jnp.cumsum(fc, axis=2)
    b_end = bcum[:, :, -1]
    q_dec = qc * jnp.exp(bcum)
    k_inv = kc * jnp.exp(-bcum)
    causal = jnp.tril(jnp.ones((L, L), bool))
    scores = jnp.where(causal[None, None, None], jnp.einsum('bcthk,bcshk->bchts', q_dec, k_inv), 0.0)
    intra = jnp.einsum('bchts,bcshv->bcthv', scores, vc)
    k_end = kc * jnp.exp(b_end[:, :, None] - bcum)
    dstate = jnp.einsum('bcshk,bcshv->bchkv', k_end, vc)

    def step(st, inp):
        ds, be = inp
        return jnp.exp(be)[..., None] * st + ds, st

    s0 = jnp.zeros((bsz, HG_H, HG_DK, HG_DV), jnp.float32)
    _, s_prev = lax.scan(step, s0, (jnp.moveaxis(dstate, 1, 0), jnp.moveaxis(b_end, 1, 0)))
    s_prev = jnp.moveaxis(s_prev, 0, 1)
    inter = jnp.einsum('bcthk,bchkv->bcthv', q_dec, s_prev)
    return (intra + inter).reshape(bsz, s, HG_H, HG_DV)


def _hgrn_mixer(h, w_in, lb, norm_w, w_out):
    bsz, s, _ = h.shape
    proj = (h @ w_in).astype(jnp.float32)
    q, f_fw, f_bw, i_in, g = jnp.split(proj, [HG_K, 2 * HG_K, 3 * HG_K, 3 * HG_K + HG_V], axis=-1)
    f_f = (lb + (1.0 - lb) * jax.nn.sigmoid(f_fw)).reshape(bsz, s, HG_H, HG_DK)
    f_b = (lb + (1.0 - lb) * jax.nn.sigmoid(f_bw)).reshape(bsz, s, HG_H, HG_DK)
    q = q.reshape(bsz, s, HG_H, HG_DK)
    v = i_in.reshape(bsz, s, HG_H, HG_DV)
    o_fwd = _hgrn_scan(q, 1.0 - f_f, jnp.log(f_f), v)
    o_bwd = _flip(_hgrn_scan(_flip(q), _flip(1.0 - f_b), _flip(jnp.log(f_b)), _flip(v)))
    y = _rmsnorm(o_fwd + o_bwd, norm_w.reshape(HG_H, HG_DV)).reshape(bsz, s, HG_V)
    y = y * jax.nn.silu(g)
    return y.astype(h.dtype) @ w_out


def _conv_ffn(h, w_in, conv_w, conv_b, w_out):
    u = _dwconv(h @ w_in, conv_w, conv_b)
    a, b = jnp.split(u, 2, axis=-1)
    return (jax.nn.silu(a) * b) @ w_out


def _trunk(x, p):
    lbs = jnp.cumsum(jax.nn.softmax(p['hgrn_lb'].astype(jnp.float32), axis=0), axis=0)
    for i in range(DEPTH):
        j = i // N_MIXERS
        kind = i % N_MIXERS
        h = _rmsnorm(x, p['ln1'][i])
        if kind == 0:
            m = _ssd_mixer(h, p['ssd_w_in'][j], p['ssd_conv_w'][j], p['ssd_conv_b'][j], p['ssd_dt_bias'][j],
                           p['ssd_a_log'][j], p['ssd_d'][j], p['ssd_norm_w'][j], p['ssd_w_out'][j])
        elif kind == 1:
            m = _diff_attn_mixer(h, p['attn_w_qkv'][j], p['attn_q_norm'][j], p['attn_k_norm'][j],
                                 p['attn_lambda'][j], p['attn_sub_norm'][j], p['attn_w_out'][j],
                                 p['rel_bias'], 0.8 - 0.6 * math.exp(-0.3 * i))
        elif kind == 2:
            m = _mlstm_mixer(h, p['mlstm_w_in'][j], p['mlstm_gate_b'][j], p['mlstm_norm_w'][j], p['mlstm_w_out'][j])
        else:
            m = _hgrn_mixer(h, p['hgrn_w_in'][j], lbs[i] - lbs[0], p['hgrn_norm_w'][j], p['hgrn_w_out'][j])
        x = x + m.astype(x.dtype)
        h = _rmsnorm(x, p['ln2'][i])
        x = x + _conv_ffn(h, p['ffn_w_in'][i], p['ffn_conv_w'][i], p['ffn_conv_b'][i], p['ffn_w_out'][i]).astype(x.dtype)
    return x


def setup_inputs(seed: int = 0) -> dict:
    key = jax.random.key(seed)
    ks = iter(jax.random.split(key, 64))
    f32 = jnp.float32

    def nrm(shape, scale):
        return scale * jax.random.normal(next(ks), shape, f32)

    L = LAYERS_PER_MIXER
    out_scale = 0.5
    dt = jnp.exp(jax.random.uniform(next(ks), (L, 2, SSD_NH), f32)
                 * (math.log(0.1) - math.log(0.001)) + math.log(0.001))
    f_base = jnp.linspace(3.0, 6.0, ML_H, dtype=f32)
    z_base = jnp.zeros((ML_H,), f32)
    gate_base = jnp.stack([z_base, f_base, z_base, f_base])[None]
    return {
        "x_prompt": nrm((BATCH, SEQ, D_MODEL), 1.0),
        "x_sample": nrm((DEC_BATCH, DEC_SEQ, D_MODEL), 1.0),
        "ln1": 1.0 + nrm((DEPTH, D_MODEL), 0.02),
        "ln2": 1.0 + nrm((DEPTH, D_MODEL), 0.02),
        "ssd_w_in": nrm((L, D_MODEL, SSD_IN), D_MODEL ** -0.5),
        "ssd_conv_w": nrm((L, SSD_CONV, SSD_CONV_CH), SSD_CONV ** -0.5),
        "ssd_conv_b": nrm((L, SSD_CONV_CH), 0.01),
        "ssd_dt_bias": dt + jnp.log(-jnp.expm1(-dt)),
        "ssd_a_log": jnp.log(jax.random.uniform(next(ks), (L, 2, SSD_NH), f32, minval=1.0, maxval=16.0)),
        "ssd_d": 1.0 + nrm((L, SSD_NH), 0.02),
        "ssd_norm_w": 1.0 + nrm((L, SSD_DI), 0.02),
        "ssd_w_out": nrm((L, SSD_DI, D_MODEL), out_scale * SSD_DI ** -0.5),
        "attn_w_qkv": nrm((L, D_MODEL, ATT_IN), D_MODEL ** -0.5),
        "attn_q_norm": 1.0 + nrm((L, ATT_DH), 0.02),
        "attn_k_norm": 1.0 + nrm((L, ATT_DH), 0.02),
        "attn_lambda": nrm((L, 4, ATT_DH), 0.1),
        "attn_sub_norm": 1.0 + nrm((L, ATT_VD), 0.02),
        "attn_w_out": nrm((L, ATT_H * ATT_VD, D_MODEL), out_scale * (ATT_H * ATT_VD) ** -0.5),
        "rel_bias": nrm((REL_BUCKETS, ATT_H), 0.5),
        "mlstm_w_in": nrm((L, D_MODEL, ML_IN), D_MODEL ** -0.5),
        "mlstm_gate_b": gate_base + nrm((L, 4, ML_H), 0.1),
        "mlstm_norm_w": 1.0 + nrm((L, ML_V), 0.02),
        "mlstm_w_out": nrm((L, ML_V, D_MODEL), out_scale * ML_V ** -0.5),
        "hgrn_w_in": nrm((L, D_MODEL, HG_IN), D_MODEL ** -0.5),
        "hgrn_lb": nrm((DEPTH, HG_K), 0.1),
        "hgrn_norm_w": 1.0 + nrm((L, HG_V), 0.02),
        "hgrn_w_out": nrm((L, HG_V, D_MODEL), out_scale * HG_V ** -0.5),
        "ffn_w_in": nrm((DEPTH, D_MODEL, 2 * FFN_D), D_MODEL ** -0.5),
        "ffn_conv_w": nrm((DEPTH, FFN_CONV, 2 * FFN_D), FFN_CONV ** -0.5),
        "ffn_conv_b": nrm((DEPTH, 2 * FFN_D), 0.01),
        "ffn_w_out": nrm((DEPTH, FFN_D, D_MODEL), out_scale * FFN_D ** -0.5),
    }


def reference(x_prompt, x_sample, ln1, ln2,
              ssd_w_in, ssd_conv_w, ssd_conv_b, ssd_dt_bias, ssd_a_log, ssd_d, ssd_norm_w, ssd_w_out,
              attn_w_qkv, attn_q_norm, attn_k_norm, attn_lambda, attn_sub_norm, attn_w_out, rel_bias,
              mlstm_w_in, mlstm_gate_b, mlstm_norm_w, mlstm_w_out,
              hgrn_w_in, hgrn_lb, hgrn_norm_w, hgrn_w_out,
              ffn_w_in, ffn_conv_w, ffn_conv_b, ffn_w_out):
    p = dict(ln1=ln1, ln2=ln2,
             ssd_w_in=ssd_w_in, ssd_conv_w=ssd_conv_w, ssd_conv_b=ssd_conv_b, ssd_dt_bias=ssd_dt_bias,
             ssd_a_log=ssd_a_log, ssd_d=ssd_d, ssd_norm_w=ssd_norm_w, ssd_w_out=ssd_w_out,
             attn_w_qkv=attn_w_qkv, attn_q_norm=attn_q_norm, attn_k_norm=attn_k_norm,
             attn_lambda=attn_lambda, attn_sub_norm=attn_sub_norm, attn_w_out=attn_w_out, rel_bias=rel_bias,
             mlstm_w_in=mlstm_w_in, mlstm_gate_b=mlstm_gate_b, mlstm_norm_w=mlstm_norm_w, mlstm_w_out=mlstm_w_out,
             hgrn_w_in=hgrn_w_in, hgrn_lb=hgrn_lb, hgrn_norm_w=hgrn_norm_w, hgrn_w_out=hgrn_w_out,
             ffn_w_in=ffn_w_in, ffn_conv_w=ffn_conv_w, ffn_conv_b=ffn_conv_b, ffn_w_out=ffn_w_out)
    y_prompt = _trunk(x_prompt, p)
    y_sample = _trunk(x_sample, p)
    return (y_prompt, y_sample)
```

```python
import functools
import math

import jax
import jax.numpy as jnp
from jax import lax
from jax.experimental import pallas as pl
from jax.experimental.pallas import tpu as pltpu

F32 = jnp.float32
BF16 = jnp.bfloat16
NORM_EPS = 1e-6
HIGHEST = lax.Precision.HIGHEST

V7X_LANES = 128
V7X_VMEM_BYTES = 64 * 1024 * 1024
VMEM_LIMIT = V7X_VMEM_BYTES - 8 * 1024 * 1024

D_MODEL = 2048
SSD_DI = 2 * D_MODEL
SSD_P = 64
SSD_NH = SSD_DI // SSD_P
SSD_G = 8
SSD_HG = SSD_NH // SSD_G
SSD_N = 128
SSD_CONV_CH = SSD_DI + 2 * SSD_G * SSD_N
SSD_L = 128
ATT_H = 8
ATT_DH = 128
ATT_VD = 256
REL_BUCKETS = 32
REL_MAX_DIST = 128
ML_H = 8
ML_DQK = 128
ML_DV = 256
ML_L = 128
ML_QK = ML_H * ML_DQK
ML_V = ML_H * ML_DV
HG_H = 16
HG_D = 128
HG_L = 64
HG_K = HG_H * HG_D
FFN_D = 5632
HALO = 16


def _params(sem, vmem=VMEM_LIMIT):
    return pltpu.CompilerParams(dimension_semantics=sem, vmem_limit_bytes=vmem)


def _sigmoid(x):
    return 1.0 / (1.0 + jnp.exp(-x))


def _silu(x):
    return x * _sigmoid(x)


def _softplus(x):
    return jnp.maximum(x, 0.0) + jnp.log(1.0 + jnp.exp(-jnp.abs(x)))


def _log_sigmoid(x):
    return jnp.minimum(x, 0.0) - jnp.log(1.0 + jnp.exp(-jnp.abs(x)))


def _rms(x, w):
    return x * lax.rsqrt(jnp.mean(x * x, axis=-1, keepdims=True) + NORM_EPS) * w


def _group_rms(x, w, width):
    parts = []
    for g in range(x.shape[-1] // width):
        sl = slice(g * width, (g + 1) * width)
        parts.append(_rms(x[:, sl], w[:, sl]))
    return jnp.concatenate(parts, axis=-1)


def _mm_body(*refs, n_in, n_vec, n_epi, has_res, prologue, epilogue, slab):
    ins = refs[:n_in]
    vecs = refs[n_in:n_in + n_vec]
    w_ref = refs[n_in + n_vec]
    p = n_in + n_vec + 1
    epis = refs[p:p + n_epi]
    p += n_epi
    res_ref = refs[p] if has_res else None
    p += int(has_res)
    o_ref = refs[p]
    if prologue is None:
        lhs = ins[0][...]
    else:
        lhs_ref = refs[p + 1]

        @pl.when(pl.program_id(1) == 0)
        def _():
            def body(r, c):
                rows = pl.ds(pl.multiple_of(r * slab, slab), slab)
                y = prologue(*[i[rows, :] for i in ins], *[v[...] for v in vecs])
                lhs_ref[rows, :] = y.astype(BF16)
                return c
            lax.fori_loop(0, lhs_ref.shape[0] // slab, body, 0)

        lhs = lhs_ref[...]
    acc = jnp.dot(lhs, w_ref[...], preferred_element_type=F32)
    if has_res:
        acc = acc + res_ref[...]
    if epilogue is not None:
        acc = epilogue(acc, *[e[...] for e in epis])
    o_ref[...] = acc.astype(o_ref.dtype)


def _mm(inputs, vecs, w, *, prologue, out_dtype, tm, tn, residual=None, epilogue=None, epi_vecs=(),
        slab=32, name="mm"):
    inputs = [a if isinstance(a, tuple) else (a, a.shape[1], 0) for a in inputs]
    t = inputs[0][0].shape[0]
    k, n = w.shape
    tm = min(tm, t)
    tn = min(tn, n)
    assert t % tm == 0 and n % tn == 0, (t, tm, n, tn)
    in_specs = [pl.BlockSpec((tm, wd), functools.partial(lambda i, j, cb: (i, cb), cb=cb)) for _, wd, cb in inputs]
    inputs = [a for a, _, _ in inputs]
    in_specs += [pl.BlockSpec(v.shape, lambda i, j: (0, 0)) for v in vecs]
    in_specs += [pl.BlockSpec((k, tn), lambda i, j: (0, j))]
    in_specs += [pl.BlockSpec((1, tn), lambda i, j: (0, j)) for _ in epi_vecs]
    args = list(inputs) + list(vecs) + [w] + list(epi_vecs)
    if residual is not None:
        in_specs.append(pl.BlockSpec((tm, tn), lambda i, j: (i, j)))
        args.append(residual)
    scratch = [] if prologue is None else [pltpu.VMEM((tm, k), BF16)]
    body = functools.partial(_mm_body, n_in=len(inputs), n_vec=len(vecs), n_epi=len(epi_vecs),
                             has_res=residual is not None, prologue=prologue, epilogue=epilogue, slab=slab)
    return pl.pallas_call(
        body,
        grid=(t // tm, n // tn),
        in_specs=in_specs,
        out_specs=pl.BlockSpec((tm, tn), lambda i, j: (i, j)),
        out_shape=jax.ShapeDtypeStruct((t, n), out_dtype),
        scratch_shapes=scratch,
        compiler_params=_params(("parallel", "arbitrary")),
        name=name,
    )(*args)


def _conv_taps(prev_ref, cur_ref, next_ref, w_ref, b_ref, xe_ref, rows):
    i = pl.program_id(1)
    last = pl.num_programs(1) - 1
    keep_prev = (i > 0).astype(F32)
    keep_next = (i < last).astype(F32)
    xe_ref[0:HALO, :] = prev_ref[0].astype(F32) * keep_prev
    xe_ref[HALO:HALO + rows, :] = cur_ref[0].astype(F32)
    xe_ref[HALO + rows:HALO + rows + HALO, :] = next_ref[0].astype(F32) * keep_next
    width = w_ref.shape[0]
    acc = b_ref[...] + w_ref[0:1, :] * xe_ref[pl.ds(HALO - width // 2, rows), :]
    for k in range(1, width):
        acc = acc + w_ref[k:k + 1, :] * xe_ref[pl.ds(HALO - width // 2 + k, rows), :]
    return acc


def _conv_silu_body(prev_ref, cur_ref, next_ref, w_ref, b_ref, o_ref, xe_ref):
    rows = cur_ref.shape[1]
    o_ref[0] = _silu(_conv_taps(prev_ref, cur_ref, next_ref, w_ref, b_ref, xe_ref, rows)).astype(o_ref.dtype)


def _conv_glu_body(pa, ca, na, pb, cb, nb, wa, ba, wb, bb, o_ref, xa_ref, xb_ref):
    rows = ca.shape[1]
    a = _conv_taps(pa, ca, na, wa, ba, xa_ref, rows)
    b = _conv_taps(pb, cb, nb, wb, bb, xb_ref, rows)
    o_ref[0] = (_silu(a) * b).astype(o_ref.dtype)


def _halo_specs(rows, tc, seq, col_off):
    nh = rows // HALO
    nblk = seq // HALO
    prev = pl.BlockSpec((1, HALO, tc), lambda b, i, j: (b, jnp.maximum(i * nh - 1, 0), j + col_off))
    cur = pl.BlockSpec((1, rows, tc), lambda b, i, j: (b, i, j + col_off))
    nxt = pl.BlockSpec((1, HALO, tc), lambda b, i, j: (b, jnp.minimum((i + 1) * nh, nblk - 1), j + col_off))
    return [prev, cur, nxt]


def _conv_silu(x, w, b, *, col0, ncols, rows=512, tc=512):
    bsz, seq, _ = x.shape
    rows = min(rows, seq)
    assert col0 % tc == 0 and ncols % tc == 0 and seq % rows == 0
    wspec = pl.BlockSpec((w.shape[0], tc), lambda b_, i, j: (0, j))
    bspec = pl.BlockSpec((1, tc), lambda b_, i, j: (0, j))
    return pl.pallas_call(
        _conv_silu_body,
        grid=(bsz, seq // rows, ncols // tc),
        in_specs=_halo_specs(rows, tc, seq, col0 // tc) + [wspec, bspec],
        out_specs=pl.BlockSpec((1, rows, tc), lambda b_, i, j: (b_, i, j)),
        out_shape=jax.ShapeDtypeStruct((bsz, seq, ncols), BF16),
        scratch_shapes=[pltpu.VMEM((rows + 2 * HALO, tc), F32)],
        compiler_params=_params(("parallel", "arbitrary", "arbitrary")),
        name="conv_silu",
    )(x, x, x, w, b)


def _conv_glu(u, w, b, *, rows=512, tc=512):
    bsz, seq, two_f = u.shape
    f = two_f // 2
    rows = min(rows, seq)
    assert f % tc == 0 and seq % rows == 0
    nf = f // tc
    wa = pl.BlockSpec((w.shape[0], tc), lambda b_, i, j: (0, j))
    wb = pl.BlockSpec((w.shape[0], tc), lambda b_, i, j: (0, j + nf))
    ba = pl.BlockSpec((1, tc), lambda b_, i, j: (0, j))
    bb = pl.BlockSpec((1, tc), lambda b_, i, j: (0, j + nf))
    return pl.pallas_call(
        _conv_glu_body,
        grid=(bsz, seq // rows, nf),
        in_specs=_halo_specs(rows, tc, seq, 0) + _halo_specs(rows, tc, seq, nf) + [wa, ba, wb, bb],
        out_specs=pl.BlockSpec((1, rows, tc), lambda b_, i, j: (b_, i, j)),
        out_shape=jax.ShapeDtypeStruct((bsz, seq, f), BF16),
        scratch_shapes=[pltpu.VMEM((rows + 2 * HALO, tc), F32), pltpu.VMEM((rows + 2 * HALO, tc), F32)],
        compiler_params=_params(("parallel", "arbitrary", "arbitrary")),
        name="conv_glu",
    )(u, u, u, u, u, u, w, b, w, b)


def _ffn(x2d, bsz, seq, ln_w, w_in, conv_w, conv_b, w_out):
    u = _mm([x2d], [ln_w], w_in, prologue=_rms, out_dtype=BF16, tm=1024, tn=1024, name="ffn_in")
    g = _conv_glu(u.reshape(bsz, seq, -1), conv_w, conv_b)
    return _mm([g.reshape(bsz * seq, -1)], [], w_out, prologue=None, out_dtype=F32, tm=512, tn=1024,
               residual=x2d, name="ffn_out")


def _cum_masks(length, reverse):
    row = lax.broadcasted_iota(jnp.int32, (length, length), 0)
    col = lax.broadcasted_iota(jnp.int32, (length, length), 1)
    if reverse:
        return col >= row, row >= col
    return col <= row, row <= col


def _lane_expand(v, heads, width):
    rows = v.shape[0]
    return jnp.concatenate([jnp.broadcast_to(v[:, h:h + 1], (rows, width)) for h in heads], axis=-1)


def _ssd_scan_body(*refs, reverse, final):
    if final:
        xc_ref, dt_ref, dtb_ref, alog_ref, z_ref, yb_ref, dsk_ref, nw_ref, o_ref, st_ref = refs
    else:
        xc_ref, dt_ref, dtb_ref, alog_ref, o_ref, st_ref = refs

    @pl.when(pl.program_id(1) == 0)
    def _():
        st_ref[...] = jnp.zeros_like(st_ref)

    L = SSD_L
    mask, mask_t = _cum_masks(L, reverse)
    d0 = SSD_NH if reverse else 0
    dt_all = _softplus(dt_ref[0] + dtb_ref[...])
    a_all = -jnp.exp(alog_ref[...]) * dt_all
    cum_all = jnp.dot(mask.astype(F32), a_all, precision=HIGHEST, preferred_element_type=F32)
    cum_t_all = jnp.dot(a_all.T, mask_t.astype(F32), precision=HIGHEST, preferred_element_type=F32)
    dt = dt_all[:, d0:d0 + SSD_NH]
    cum = cum_all[:, d0:d0 + SSD_NH]
    cum_t = cum_t_all[d0:d0 + SSD_NH, :]
    total = cum[0:1, :] if reverse else cum[L - 1:L, :]
    w_state = dt * jnp.exp(total - cum)
    e_cum = jnp.exp(cum)
    chunk_decay = jnp.exp(total)

    gw = SSD_HG * SSD_P
    bc0 = SSD_DI
    for g in range(SSD_G):
        heads = range(g * SSD_HG, (g + 1) * SSD_HG)
        b_g = xc_ref[0, :, bc0 + g * SSD_N:bc0 + (g + 1) * SSD_N]
        c_g = xc_ref[0, :, bc0 + (SSD_G + g) * SSD_N:bc0 + (SSD_G + g + 1) * SSD_N]
        xs_g = xc_ref[0, :, g * gw:(g + 1) * gw].astype(F32)
        cb = lax.dot_general(c_g, b_g, (((1,), (1,)), ((), ())), preferred_element_type=F32)
        xdt = (xs_g * _lane_expand(dt, heads, SSD_P)).astype(BF16)
        xw = (xs_g * _lane_expand(w_state, heads, SSD_P)).astype(BF16)
        h_prev = st_ref[g]
        y_g = jnp.dot(c_g, h_prev.astype(BF16), preferred_element_type=F32) * _lane_expand(e_cum, heads, SSD_P)
        states = lax.dot_general(b_g, xw, (((0,), (0,)), ((), ())), preferred_element_type=F32)
        st_ref[g] = h_prev * _lane_expand(chunk_decay, heads, SSD_P) + states
        y_diag = []
        for hh, h in enumerate(heads):
            seg = cum[:, h:h + 1] - cum_t[h:h + 1, :]
            m = (cb * jnp.exp(jnp.where(mask, seg, -jnp.inf))).astype(BF16)
            y_diag.append(jnp.dot(m, xdt[:, hh * SSD_P:(hh + 1) * SSD_P], preferred_element_type=F32))
        y_g = y_g + jnp.concatenate(y_diag, axis=-1)
        cols = slice(g * gw, (g + 1) * gw)
        if final:
            y_g = y_g + yb_ref[0, :, cols] + dsk_ref[:, cols] * xs_g
            o_ref[0, :, cols] = y_g * _silu(z_ref[0, :, cols].astype(F32))
        else:
            o_ref[0, :, cols] = y_g
    if final:
        o_ref[0] = _rms(o_ref[0], nw_ref[...])


def _ssd_scan(xc, dt_raw, dt_bias, a_log, *, reverse, tail=None):
    bsz, seq, _ = xc.shape
    nc = seq // SSD_L
    final = tail is not None

    def cidx(c):
        return nc - 1 - c if reverse else c

    row3 = lambda b, c: (b, cidx(c), 0)
    vec = lambda b, c: (0, 0)
    in_specs = [pl.BlockSpec((1, SSD_L, xc.shape[2]), row3), pl.BlockSpec((1, SSD_L, 2 * SSD_NH), row3),
                pl.BlockSpec((1, 2 * SSD_NH), vec), pl.BlockSpec((1, 2 * SSD_NH), vec)]
    args = [xc, dt_raw, dt_bias, a_log]
    if final:
        z, y_other, d_skip, norm_w = tail
        in_specs += [pl.BlockSpec((1, SSD_L, SSD_DI), row3), pl.BlockSpec((1, SSD_L, SSD_DI), row3),
                     pl.BlockSpec((1, SSD_DI), vec), pl.BlockSpec((1, SSD_DI), vec)]
        args += [z, y_other, d_skip, norm_w]
    return pl.pallas_call(
        functools.partial(_ssd_scan_body, reverse=reverse, final=final),
        grid=(bsz, nc),
        in_specs=in_specs,
        out_specs=pl.BlockSpec((1, SSD_L, SSD_DI), row3),
        out_shape=jax.ShapeDtypeStruct((bsz, seq, SSD_DI), F32),
        scratch_shapes=[pltpu.VMEM((SSD_G, SSD_N, SSD_HG * SSD_P), F32)],
        compiler_params=_params(("parallel", "arbitrary")),
        name="ssd_scan_rev" if reverse else "ssd_scan_fwd",
    )(*args)


def _ssd_layer(x2d, bsz, seq, p):
    t = bsz * seq
    proj = _mm([x2d], [p["ln1"][0]], p["ssd_w_in_zx"], prologue=_rms, out_dtype=BF16, tm=1024, tn=1024,
               name="ssd_in")
    dt_raw = _mm([x2d], [p["ln1"][0]], p["ssd_w_in_dt"], prologue=_rms, out_dtype=F32, tm=1024, tn=128,
                 name="ssd_in_dt")
    proj3 = proj.reshape(bsz, seq, -1)
    xc = _conv_silu(proj3, p["ssd_conv_w"], p["ssd_conv_b"], col0=SSD_DI, ncols=SSD_CONV_CH)
    dt3 = dt_raw.reshape(bsz, seq, -1)
    y_b = _ssd_scan(xc, dt3, p["ssd_dt_bias"], p["ssd_a_log"], reverse=True)
    y = _ssd_scan(xc, dt3, p["ssd_dt_bias"], p["ssd_a_log"], reverse=False,
                  tail=(proj3, y_b, p["ssd_d"], p["ssd_norm_w"]))
    return _mm([y.reshape(t, -1)], [], p["ssd_w_out"], prologue=_identity, out_dtype=F32, tm=512, tn=1024,
               residual=x2d, name="ssd_out")


def _identity(v):
    return v


ATT_T = 256


def _rel_bucket(rel):
    half = REL_BUCKETS // 2
    max_exact = half // 2
    ret = jnp.where(rel > 0, half, 0)
    n = jnp.abs(rel)
    nf = jnp.maximum(n, 1).astype(jnp.float32)
    large = max_exact + (jnp.log(nf / max_exact) / math.log(REL_MAX_DIST / max_exact)
                         * (half - max_exact)).astype(jnp.int32)
    large = jnp.minimum(large, half - 1)
    return ret + jnp.where(n < max_exact, n, large)


def _bias_tables(rel_bias, t):
    table = rel_bias.astype(F32)
    q = jnp.arange(t, dtype=jnp.int32)
    rel = (jnp.arange(-1, 2, dtype=jnp.int32)[:, None, None] * t + q[None, None, :] - q[None, :, None])
    band = jnp.transpose(table[_rel_bucket(rel)], (3, 0, 1, 2))
    far = jnp.transpose(table[_rel_bucket(jnp.array([-(t + 1), t + 1], jnp.int32))], (1, 0))
    return band, far


def _attn_body(far_ref, q_ref, k_ref, v_ref, band_ref, lam_ref, sn_ref, o_ref, m_ref, l_ref, acc_ref,
               *, t, nk, lam_init):
    h = pl.program_id(1)
    i = pl.program_id(2)
    m_ref[...] = jnp.full_like(m_ref, -jnp.inf)
    l_ref[...] = jnp.zeros_like(l_ref)
    acc_ref[...] = jnp.zeros_like(acc_ref)
    q = q_ref[0]
    scale = ATT_DH ** -0.5

    def step(j, bias):
        rows = pl.ds(pl.multiple_of(j * t, t), t)
        kt = k_ref[0, rows, :]
        vt = v_ref[0, rows, :]
        for mp in range(2):
            cols = slice(mp * ATT_DH, (mp + 1) * ATT_DH)
            s = lax.dot_general(q[:, cols], kt[:, cols], (((1,), (1,)), ((), ())),
                                preferred_element_type=F32) * scale + bias
            m_prev = m_ref[mp]
            m_new = jnp.maximum(m_prev, jnp.max(s, axis=-1, keepdims=True))
            alpha = jnp.exp(m_prev - m_new)
            pr = jnp.exp(s - m_new)
            l_ref[mp] = alpha * l_ref[mp] + jnp.sum(pr, axis=-1, keepdims=True)
            acc_ref[mp] = alpha * acc_ref[mp] + jnp.dot(pr.astype(BF16), vt, preferred_element_type=F32)
            m_ref[mp] = m_new

    def far_loop(lo, hi, side):
        def body(j, c):
            step(j, far_ref[h, side])
            return c
        lax.fori_loop(lo, hi, body, 0)

    far_loop(0, jnp.maximum(i - 1, 0), 0)
    for d in (-1, 0, 1):
        j = i + d

        @pl.when(jnp.logical_and(j >= 0, j < nk))
        def _():
            step(j, band_ref[0, d + 1])

    far_loop(jnp.minimum(i + 2, nk), nk, 1)

    lp = lam_ref[...]
    lam = (jnp.exp(jnp.sum(lp[0:1] * lp[1:2], axis=-1, keepdims=True))
           - jnp.exp(jnp.sum(lp[2:3] * lp[3:4], axis=-1, keepdims=True)) + lam_init)
    o = acc_ref[0] / l_ref[0] - lam * (acc_ref[1] / l_ref[1])
    o_ref[0] = (_rms(o, sn_ref[...]) * (1.0 - lam_init)).astype(o_ref.dtype)


def _diff_attn(qk, v, band, far, lam_p, sub_norm, lam_init):
    bsz, seq, _ = qk.shape
    t = min(ATT_T, seq)
    assert seq % t == 0 and (t >= REL_MAX_DIST or seq == t)
    nk = seq // t
    hw = 2 * ATT_DH
    return pl.pallas_call(
        functools.partial(_attn_body, t=t, nk=nk, lam_init=lam_init),
        grid=(bsz, ATT_H, nk),
        in_specs=[pl.BlockSpec(memory_space=pltpu.SMEM),
                  pl.BlockSpec((1, t, hw), lambda b, h, i: (b, i, h)),
                  pl.BlockSpec((1, seq, hw), lambda b, h, i: (b, 0, ATT_H + h)),
                  pl.BlockSpec((1, seq, ATT_VD), lambda b, h, i: (b, 0, h)),
                  pl.BlockSpec((1, 3, t, t), lambda b, h, i: (h, 0, 0, 0)),
                  pl.BlockSpec(lam_p.shape, lambda b, h, i: (0, 0)),
                  pl.BlockSpec(sub_norm.shape, lambda b, h, i: (0, 0))],
        out_specs=pl.BlockSpec((1, t, ATT_VD), lambda b, h, i: (b, i, h)),
        out_shape=jax.ShapeDtypeStruct((bsz, seq, ATT_H * ATT_VD), BF16),
        scratch_shapes=[pltpu.VMEM((2, t, 1), F32), pltpu.VMEM((2, t, 1), F32), pltpu.VMEM((2, t, ATT_VD), F32)],
        compiler_params=_params(("parallel", "parallel", "arbitrary")),
        name="diff_attn",
    )(far, qk, qk, v, band, lam_p, sub_norm)


def _attn_layer(x2d, bsz, seq, p, layer=1):
    lam_init = 0.8 - 0.6 * math.exp(-0.3 * layer)
    ln = p["ln1"][layer]
    qk = _mm([x2d], [ln], p["attn_w_qk"], prologue=_rms, out_dtype=BF16, tm=1024, tn=1024,
             epilogue=functools.partial(_group_rms, width=ATT_DH), epi_vecs=[p["attn_qk_norm"]], name="attn_qk")
    v = _mm([x2d], [ln], p["attn_w_v"], prologue=_rms, out_dtype=BF16, tm=1024, tn=1024, name="attn_v")
    band, far = _bias_tables(p["rel_bias"], min(ATT_T, seq))
    o = _diff_attn(qk.reshape(bsz, seq, -1), v.reshape(bsz, seq, -1), band, far, p["attn_lambda"],
                   p["attn_sub_norm"], lam_init)
    return _mm([o.reshape(bsz * seq, -1)], [], p["attn_w_out"], prologue=None, out_dtype=F32, tm=512, tn=1024,
               residual=x2d, name="attn_out")


def _mlstm_body(q_ref, k_ref, v_ref, gc_ref, gr_ref, gbc_ref, gbr_ref, o_ref, c_ref, n_ref, m_ref, *, reverse):
    @pl.when(pl.program_id(1) == 0)
    def _():
        c_ref[...] = jnp.zeros_like(c_ref)
        n_ref[...] = jnp.zeros_like(n_ref)
        m_ref[...] = jnp.zeros_like(m_ref)

    L = ML_L
    mask, mask_t = _cum_masks(L, reverse)
    g_col = gc_ref[0] + gbc_ref[...]
    g_row = gr_ref[0] + gbr_ref[...]
    i0 = 2 * ML_H if reverse else 0
    f0 = i0 + ML_H
    lf_col = _log_sigmoid(g_col[:, f0:f0 + ML_H])
    lf_row = _log_sigmoid(g_row[f0:f0 + ML_H, :])
    cum_col = jnp.dot(mask.astype(F32), lf_col, precision=HIGHEST, preferred_element_type=F32)
    cum_row = jnp.dot(lf_row, mask_t.astype(F32), precision=HIGHEST, preferred_element_type=F32)
    scale = ML_DQK ** -0.5
    for h in range(ML_H):
        q = q_ref[0, :, h * ML_DQK:(h + 1) * ML_DQK]
        k = (k_ref[0, :, h * ML_DQK:(h + 1) * ML_DQK].astype(F32) * scale)
        kb = k.astype(BF16)
        v = v_ref[0, :, h * ML_DV:(h + 1) * ML_DV]
        b_col = cum_col[:, h:h + 1]
        b_row = cum_row[h:h + 1, :]
        ig_col = g_col[:, i0 + h:i0 + h + 1]
        ig_row = g_row[i0 + h:i0 + h + 1, :]
        g_end = b_col[0:1, :] if reverse else b_col[L - 1:L, :]
        m_prev = m_ref[h][:, 0:1]
        c_prev = c_ref[h]
        n_prev = n_ref[h]
        d_ts = jnp.where(mask, b_col - b_row + ig_row, -jnp.inf)
        a_t = b_col + m_prev
        m_t = jnp.maximum(a_t, jnp.max(d_ts, axis=-1, keepdims=True))
        qk = lax.dot_general(q, kb, (((1,), (1,)), ((), ())), preferred_element_type=F32)
        scores = qk * jnp.exp(d_ts - m_t)
        inter = jnp.exp(a_t - m_t)
        num = (jnp.dot(scores.astype(BF16), v, preferred_element_type=F32)
               + inter * jnp.dot(q, c_prev.astype(BF16), preferred_element_type=F32))
        den = (jnp.sum(scores, axis=-1, keepdims=True)
               + inter * jnp.sum(q.astype(F32) * n_prev, axis=-1, keepdims=True))
        o_ref[0, :, h * ML_DV:(h + 1) * ML_DV] = num / jnp.maximum(jnp.abs(den), jnp.exp(-m_t))
        w_row = g_end - b_row + ig_row
        w_col = g_end - b_col + ig_col
        m_new = jnp.maximum(g_end + m_prev, jnp.max(w_row, axis=-1, keepdims=True))
        decay = jnp.exp(g_end + m_prev - m_new)
        kw = k * jnp.exp(w_col - m_new)
        c_ref[h] = decay * c_prev + lax.dot_general(kw.astype(BF16), v, (((0,), (0,)), ((), ())),
                                                    preferred_element_type=F32)
        n_ref[h] = decay * n_prev + jnp.sum(kw, axis=0, keepdims=True)
        m_ref[h] = jnp.broadcast_to(m_new, (1, V7X_LANES))


def _mlstm_scan(proj, g_col, g_row, gb_col, gb_row, *, reverse):
    bsz, seq, _ = proj.shape
    nc = seq // ML_L

    def cidx(c):
        return nc - 1 - c if reverse else c

    return pl.pallas_call(
        functools.partial(_mlstm_body, reverse=reverse),
        grid=(bsz, nc),
        in_specs=[pl.BlockSpec((1, ML_L, ML_QK), lambda b, c: (b, cidx(c), 0)),
                  pl.BlockSpec((1, ML_L, ML_QK), lambda b, c: (b, cidx(c), 1)),
                  pl.BlockSpec((1, ML_L, ML_V), lambda b, c: (b, cidx(c), 1)),
                  pl.BlockSpec((1, ML_L, 4 * ML_H), lambda b, c: (b, cidx(c), 0)),
                  pl.BlockSpec((1, 4 * ML_H, ML_L), lambda b, c: (b, 0, cidx(c))),
                  pl.BlockSpec((1, 4 * ML_H), lambda b, c: (0, 0)),
                  pl.BlockSpec((4 * ML_H, 1), lambda b, c: (0, 0))],
        out_specs=pl.BlockSpec((1, ML_L, ML_V), lambda b, c: (b, cidx(c), 0)),
        out_shape=jax.ShapeDtypeStruct((bsz, seq, ML_V), F32),
        scratch_shapes=[pltpu.VMEM((ML_H, ML_DQK, ML_DV), F32), pltpu.VMEM((ML_H, 1, ML_DQK), F32),
                        pltpu.VMEM((ML_H, 1, V7X_LANES), F32)],
        compiler_params=_params(("parallel", "arbitrary")),
        name="mlstm_scan_rev" if reverse else "mlstm_scan_fwd",
    )(proj, proj, proj, g_col, g_row, gb_col, gb_row)


def _mlstm_combine(h_f, h_b, o, norm_w):
    return _group_rms(h_f + h_b, norm_w, ML_DV) * _sigmoid(o.astype(F32))


def _mlstm_layer(x2d, bsz, seq, p, layer=2):
    ln = p["ln1"][layer]
    proj = _mm([x2d], [ln], p["mlstm_w_in"], prologue=_rms, out_dtype=BF16, tm=1024, tn=1024, name="mlstm_in")
    gates = _mm([x2d], [ln], p["mlstm_w_gates"], prologue=_rms, out_dtype=F32, tm=1024, tn=4 * ML_H,
                name="mlstm_in_gates")
    g_col = gates.reshape(bsz, seq, -1)
    g_row = jnp.swapaxes(g_col, 1, 2)
    proj3 = proj.reshape(bsz, seq, -1)
    h_f = _mlstm_scan(proj3, g_col, g_row, p["mlstm_gate_b_col"], p["mlstm_gate_b_row"], reverse=False)
    h_b = _mlstm_scan(proj3, g_col, g_row, p["mlstm_gate_b_col"], p["mlstm_gate_b_row"], reverse=True)
    t = bsz * seq
    return _mm([h_f.reshape(t, -1), h_b.reshape(t, -1), (proj, ML_V, (2 * ML_QK + ML_V) // ML_V)],
               [p["mlstm_norm_w"]], p["mlstm_w_out"], prologue=_mlstm_combine, out_dtype=F32, tm=512, tn=1024,
               residual=x2d, name="mlstm_out")


def _hgrn_body(q_ref, f_ref, v_ref, lb_ref, o_ref, st_ref, *, reverse, layer):
    @pl.when(pl.program_id(1) == 0)
    def _():
        st_ref[...] = jnp.zeros_like(st_ref)

    L = HG_L
    mask, _ = _cum_masks(L, reverse)
    lbp = lb_ref[...]
    e = jnp.exp(lbp - jnp.max(lbp, axis=0, keepdims=True))
    lb = jnp.sum(e[1:layer + 1], axis=0, keepdims=True) / jnp.sum(e, axis=0, keepdims=True)
    f = lb + (1.0 - lb) * _sigmoid(f_ref[0].astype(F32))
    logf = jnp.log(f)
    bcum = jnp.dot(mask.astype(F32), logf, precision=HIGHEST, preferred_element_type=F32)
    b_end = bcum[0:1, :] if reverse else bcum[L - 1:L, :]
    qf = q_ref[0].astype(F32)
    k = 1.0 - f
    q_dec = (qf * jnp.exp(bcum)).astype(BF16)
    k_inv = (k * jnp.exp(-bcum)).astype(BF16)
    k_end = (k * jnp.exp(b_end - bcum)).astype(BF16)
    chunk_decay = jnp.exp(b_end)
    for h in range(HG_H):
        cols = slice(h * HG_D, (h + 1) * HG_D)
        v = v_ref[0, :, cols]
        s_prev = st_ref[h]
        scores = lax.dot_general(q_dec[:, cols], k_inv[:, cols], (((1,), (1,)), ((), ())),
                                 preferred_element_type=F32)
        scores = jnp.where(mask, scores, 0.0).astype(BF16)
        intra = jnp.dot(scores, v, preferred_element_type=F32)
        inter = lax.dot_general(q_dec[:, cols], s_prev.astype(BF16), (((1,), (1,)), ((), ())),
                                preferred_element_type=F32)
        o_ref[0, :, cols] = intra + inter
        st_ref[h] = s_prev * chunk_decay[:, cols] + lax.dot_general(
            v, k_end[:, cols], (((0,), (0,)), ((), ())), preferred_element_type=F32)


def _hgrn_scan(proj, hgrn_lb, *, reverse, layer):
    bsz, seq, _ = proj.shape
    nc = seq // HG_L

    def cidx(c):
        return nc - 1 - c if reverse else c

    fblk = 2 if reverse else 1
    return pl.pallas_call(
        functools.partial(_hgrn_body, reverse=reverse, layer=layer),
        grid=(bsz, nc),
        in_specs=[pl.BlockSpec((1, HG_L, HG_K), lambda b, c: (b, cidx(c), 0)),
                  pl.BlockSpec((1, HG_L, HG_K), lambda b, c: (b, cidx(c), fblk)),
                  pl.BlockSpec((1, HG_L, HG_K), lambda b, c: (b, cidx(c), 3)),
                  pl.BlockSpec(hgrn_lb.shape, lambda b, c: (0, 0))],
        out_specs=pl.BlockSpec((1, HG_L, HG_K), lambda b, c: (b, cidx(c), 0)),
        out_shape=jax.ShapeDtypeStruct((bsz, seq, HG_K), F32),
        scratch_shapes=[pltpu.VMEM((HG_H, HG_D, HG_D), F32)],
        compiler_params=_params(("parallel", "arbitrary")),
        name="hgrn_scan_rev" if reverse else "hgrn_scan_fwd",
    )(proj, proj, proj, hgrn_lb)


def _hgrn_combine(o_f, o_b, g, norm_w):
    return _group_rms(o_f + o_b, norm_w, HG_D) * _silu(g.astype(F32))


def _hgrn_layer(x2d, bsz, seq, p, layer=3):
    ln = p["ln1"][layer]
    proj = _mm([x2d], [ln], p["hgrn_w_in"], prologue=_rms, out_dtype=BF16, tm=1024, tn=1024, name="hgrn_in")
    proj3 = proj.reshape(bsz, seq, -1)
    o_f = _hgrn_scan(proj3, p["hgrn_lb"], reverse=False, layer=layer)
    o_b = _hgrn_scan(proj3, p["hgrn_lb"], reverse=True, layer=layer)
    t = bsz * seq
    return _mm([o_f.reshape(t, -1), o_b.reshape(t, -1), (proj, HG_K, 4)], [p["hgrn_norm_w"]], p["hgrn_w_out"],
               prologue=_hgrn_combine, out_dtype=F32, tm=512, tn=1024, residual=x2d, name="hgrn_out")


def _trunk(x, p):
    bsz, seq, d = x.shape
    x2d = x.reshape(bsz * seq, d)
    layers = (_ssd_layer, _attn_layer, _mlstm_layer, _hgrn_layer)
    for i, mixer in enumerate(layers):
        x2d = mixer(x2d, bsz, seq, p)
        x2d = _ffn(x2d, bsz, seq, p["ln2"][i], p["ffn_w_in"][i], p["ffn_conv_w"][i], p["ffn_conv_b"][i],
                   p["ffn_w_out"][i])
    return x2d.reshape(bsz, seq, d)


def _prep_params(q):
    row = lambda v: v.reshape(1, -1).astype(F32)
    p = {
        "ln1": [row(q["ln1"][i]) for i in range(4)],
        "ln2": [row(q["ln2"][i]) for i in range(4)],
        "ssd_w_in_zx": q["ssd_w_in"][0][:, :SSD_DI + SSD_CONV_CH].astype(BF16),
        "ssd_w_in_dt": q["ssd_w_in"][0][:, SSD_DI + SSD_CONV_CH:].astype(BF16),
        "ssd_conv_w": q["ssd_conv_w"][0].astype(F32),
        "ssd_conv_b": row(q["ssd_conv_b"][0]),
        "ssd_dt_bias": row(q["ssd_dt_bias"][0]),
        "ssd_a_log": row(q["ssd_a_log"][0]),
        "ssd_d": row(jnp.repeat(q["ssd_d"][0], SSD_P)),
        "ssd_norm_w": row(q["ssd_norm_w"][0]),
        "ssd_w_out": q["ssd_w_out"][0].astype(BF16),
        "attn_w_qk": q["attn_w_qkv"][0][:, :4 * ATT_H * ATT_DH].astype(BF16),
        "attn_w_v": q["attn_w_qkv"][0][:, 4 * ATT_H * ATT_DH:].astype(BF16),
        "attn_qk_norm": row(jnp.concatenate([jnp.tile(q["attn_q_norm"][0], 2 * ATT_H),
                                             jnp.tile(q["attn_k_norm"][0], 2 * ATT_H)])),
        "attn_lambda": q["attn_lambda"][0].astype(F32),
        "attn_sub_norm": row(q["attn_sub_norm"][0]),
        "attn_w_out": q["attn_w_out"][0].astype(BF16),
        "rel_bias": q["rel_bias"],
        "mlstm_w_in": q["mlstm_w_in"][0][:, :2 * ML_QK + 2 * ML_V].astype(BF16),
        "mlstm_w_gates": q["mlstm_w_in"][0][:, 2 * ML_QK + 2 * ML_V:].astype(BF16),
        "mlstm_gate_b_col": row(q["mlstm_gate_b"][0]),
        "mlstm_gate_b_row": q["mlstm_gate_b"][0].reshape(-1, 1).astype(F32),
        "mlstm_norm_w": row(q["mlstm_norm_w"][0]),
        "mlstm_w_out": q["mlstm_w_out"][0].astype(BF16),
        "hgrn_w_in": q["hgrn_w_in"][0].astype(BF16),
        "hgrn_lb": q["hgrn_lb"].astype(F32),
        "hgrn_norm_w": row(q["hgrn_norm_w"][0]),
        "hgrn_w_out": q["hgrn_w_out"][0].astype(BF16),
        "ffn_w_in": [q["ffn_w_in"][i].astype(BF16) for i in range(4)],
        "ffn_conv_w": [q["ffn_conv_w"][i].astype(F32) for i in range(4)],
        "ffn_conv_b": [row(q["ffn_conv_b"][i]) for i in range(4)],
        "ffn_w_out": [q["ffn_w_out"][i].astype(BF16) for i in range(4)],
    }
    return p


def kernel(x_prompt, x_sample, ln1, ln2, ssd_w_in, ssd_conv_w, ssd_conv_b, ssd_dt_bias, ssd_a_log, ssd_d,
           ssd_norm_w, ssd_w_out, attn_w_qkv, attn_q_norm, attn_k_norm, attn_lambda, attn_sub_norm, attn_w_out,
           rel_bias, mlstm_w_in, mlstm_gate_b, mlstm_norm_w, mlstm_w_out, hgrn_w_in, hgrn_lb, hgrn_norm_w,
           hgrn_w_out, ffn_w_in, ffn_conv_w, ffn_conv_b, ffn_w_out):
    p = _prep_params(dict(
        ln1=ln1, ln2=ln2, ssd_w_in=ssd_w_in, ssd_conv_w=ssd_conv_w, ssd_conv_b=ssd_conv_b,
        ssd_dt_bias=ssd_dt_bias, ssd_a_log=ssd_a_log, ssd_d=ssd_d, ssd_norm_w=ssd_norm_w, ssd_w_out=ssd_w_out,
        attn_w_qkv=attn_w_qkv, attn_q_norm=attn_q_norm, attn_k_norm=attn_k_norm, attn_lambda=attn_lambda,
        attn_sub_norm=attn_sub_norm, attn_w_out=attn_w_out, rel_bias=rel_bias, mlstm_w_in=mlstm_w_in,
        mlstm_gate_b=mlstm_gate_b, mlstm_norm_w=mlstm_norm_w, mlstm_w_out=mlstm_w_out, hgrn_w_in=hgrn_w_in,
        hgrn_lb=hgrn_lb, hgrn_norm_w=hgrn_norm_w, hgrn_w_out=hgrn_w_out, ffn_w_in=ffn_w_in,
        ffn_conv_w=ffn_conv_w, ffn_conv_b=ffn_conv_b, ffn_w_out=ffn_w_out))
    return _trunk(x_prompt, p), _trunk(x_sample, p)
```
